```python
import math
import jax, jax.numpy as jnp
from jax import lax
import numpy as np

D_MODEL = 1024
BATCH = 16
SEQ = 2048
DEPTH = 4

CHUNK = 64
N_META = 16
SSD_PAD = (-N_META) % CHUNK
D_MIX = 2 * D_MODEL
D_A = D_MIX // 4
D_B = D_MIX // 2
D_C = D_MIX // 4
CONV_A_K = 3
SSM_HEAD_DIM = 64
SSM_HEADS = D_B // SSM_HEAD_DIM
SSM_GROUPS = 2
SSM_HPG = SSM_HEADS // SSM_GROUPS
SSM_STATE = 128
SSM_CONV_K = 4
CONF_K = 31
NORM_EPS = 1e-6
LN_EPS = 1e-5

IN_SIZES = [D_A, D_A, D_A, D_A,
            D_B, D_B, SSM_GROUPS * SSM_STATE, SSM_GROUPS * SSM_STATE, SSM_HEADS,
            D_C, D_C, D_C]
N_IN = int(sum(IN_SIZES))
IN_SPLITS = [int(v) for v in np.cumsum(IN_SIZES)[:-1]]

kernel_name = "hybrid_conv_ssd_conformer_trunk"


def rmsnorm(x, g):
    xf = x.astype(jnp.float32)
    y = xf * lax.rsqrt(jnp.mean(xf * xf, axis=-1, keepdims=True) + NORM_EPS)
    return (y * g.astype(jnp.float32)).astype(x.dtype)


def layernorm(x, g, b):
    xf = x.astype(jnp.float32)
    mu = jnp.mean(xf, axis=-1, keepdims=True)
    var = jnp.mean(jnp.square(xf - mu), axis=-1, keepdims=True)
    y = (xf - mu) * lax.rsqrt(var + LN_EPS)
    return (y * g.astype(jnp.float32) + b.astype(jnp.float32)).astype(x.dtype)


def causal_dwconv(x, w, b=None):
    k, c = w.shape
    out = lax.conv_general_dilated(
        x, w.astype(x.dtype).reshape(k, 1, c), window_strides=(1,), padding=[(k - 1, 0)],
        dimension_numbers=('NWC', 'WIO', 'NWC'), feature_group_count=c)
    if b is not None:
        out = out + b.astype(x.dtype)
    return out


def ssd_scan(x, b_mat, c_mat, dt_raw, dt_bias, a_log, d_skip):
    in_dtype = x.dtype
    f32 = jnp.float32
    bsz, l, _ = x.shape
    dt = jax.nn.softplus(dt_raw.astype(f32) + dt_bias.astype(f32))
    a = -jnp.exp(a_log.astype(f32))
    xg = x.astype(f32).reshape(bsz, l, SSM_GROUPS, SSM_HPG, SSM_HEAD_DIM)
    dtg = dt.reshape(bsz, l, SSM_GROUPS, SSM_HPG)
    xdt = xg * dtg[..., None]
    da = dtg * a.reshape(SSM_GROUPS, SSM_HPG)
    bm = b_mat.astype(f32).reshape(bsz, l, SSM_GROUPS, SSM_STATE)
    cm = c_mat.astype(f32).reshape(bsz, l, SSM_GROUPS, SSM_STATE)

    def front_pad(t):
        return jnp.pad(t, [(0, 0), (SSD_PAD, 0)] + [(0, 0)] * (t.ndim - 2))

    xdt, da, bm, cm = front_pad(xdt), front_pad(da), front_pad(bm), front_pad(cm)
    lp = l + SSD_PAD
    nc = lp // CHUNK
    xdt = xdt.reshape(bsz, nc, CHUNK, SSM_GROUPS, SSM_HPG, SSM_HEAD_DIM)
    bm = bm.reshape(bsz, nc, CHUNK, SSM_GROUPS, SSM_STATE)
    cm = cm.reshape(bsz, nc, CHUNK, SSM_GROUPS, SSM_STATE)
    da = da.reshape(bsz, nc, CHUNK, SSM_GROUPS, SSM_HPG).transpose(0, 3, 4, 1, 2)
    a_cum = jnp.cumsum(da, axis=-1)

    causal = jnp.tril(jnp.ones((CHUNK, CHUNK), dtype=bool))
    seg = a_cum[..., :, None] - a_cum[..., None, :]
    decay = jnp.exp(jnp.where(causal, seg, -jnp.inf))
    cb = jnp.einsum('bclgn,bcsgn->bcgls', cm, bm)
    y_diag = jnp.einsum('bcgls,bgecls,bcsgep->bclgep', cb, decay, xdt)

    decay_states = jnp.exp(a_cum[..., -1:] - a_cum)
    states = jnp.einsum('bclgn,bgecl,bclgep->bcgepn', bm, decay_states, xdt)
    chunk_decay = jnp.exp(a_cum[..., -1])

    def step(carry, inp):
        st, dec = inp
        new = carry * dec[..., None, None] + st
        return new, carry

    init = jnp.zeros((bsz, SSM_GROUPS, SSM_HPG, SSM_HEAD_DIM, SSM_STATE), f32)
    _, prev = lax.scan(step, init, (jnp.moveaxis(states, 1, 0), jnp.moveaxis(chunk_decay, 3, 0)))
    prev = jnp.moveaxis(prev, 0, 1)

    y_off = jnp.einsum('bclgn,bcgepn,bgecl->bclgep', cm, prev, jnp.exp(a_cum))
    y = (y_diag + y_off).reshape(bsz, lp, SSM_GROUPS, SSM_HPG, SSM_HEAD_DIM)[:, SSD_PAD:]
    y = y + xg * d_skip.astype(f32).reshape(SSM_GROUPS, SSM_HPG)[..., None]
    return y.reshape(bsz, l, D_B).astype(in_dtype)


def hybrid_mixer(h, w_in, w_out, conv_a_w, ssm_conv_w, ssm_conv_b, dt_bias, a_log, d_skip,
                 ssm_norm_g, conf_conv_w, conf_conv_b, conf_ln_g, conf_ln_b):
    proj = jnp.einsum('bld,dn->bln', h, w_in.astype(h.dtype))
    (a_b, a_c, a_x, a_z, b_z, b_x, b_bm, b_cm, b_dt, c_a, c_g, c_z) = jnp.split(proj, IN_SPLITS, axis=-1)

    y_a = a_b * causal_dwconv(a_c * a_x, conv_a_w) * jax.nn.silu(a_z)

    xbc = jax.nn.silu(causal_dwconv(jnp.concatenate([b_x, b_bm, b_cm], axis=-1), ssm_conv_w, ssm_conv_b))
    xs, bs, cs = jnp.split(xbc, [D_B, D_B + SSM_GROUPS * SSM_STATE], axis=-1)
    y_b = ssd_scan(xs, bs, cs, b_dt, dt_bias, a_log, d_skip)
    y_b = rmsnorm(y_b * jax.nn.silu(b_z), ssm_norm_g)

    u = c_a * jax.nn.sigmoid(c_g)
    u = causal_dwconv(u, conf_conv_w, conf_conv_b)
    u = layernorm(u, conf_ln_g, conf_ln_b)
    y_c = jax.nn.silu(u) * jax.nn.silu(c_z)

    y = jnp.concatenate([y_a, y_b, y_c], axis=-1)
    return jnp.einsum('blm,md->bld', y, w_out.astype(y.dtype))


def setup_inputs(seed: int = 0) -> dict:
    key = jax.random.key(seed)
    ks = jax.random.split(key, 20)
    f32 = jnp.float32
    x = jax.random.normal(ks[0], (BATCH, SEQ, D_MODEL), f32)
    meta = jax.random.normal(ks[1], (N_META, D_MODEL), f32)
    pre_g = 1.0 + 0.05 * jax.random.normal(ks[2], (DEPTH, D_MODEL), f32)
    post_g = 1.0 + 0.05 * jax.random.normal(ks[3], (DEPTH, D_MODEL), f32)
    w_in = jax.random.normal(ks[4], (DEPTH, D_MODEL, N_IN), f32) * D_MODEL ** -0.5
    w_out = jax.random.normal(ks[5], (DEPTH, D_MIX, D_MODEL), f32) * D_MIX ** -0.5
    conv_a_w = jax.random.normal(ks[6], (DEPTH, CONV_A_K, D_A), f32) * CONV_A_K ** -0.5
    n_xbc = D_B + 2 * SSM_GROUPS * SSM_STATE
    ssm_conv_w = jax.random.normal(ks[7], (DEPTH, SSM_CONV_K, n_xbc), f32) * SSM_CONV_K ** -0.5
    ssm_conv_b = 0.02 * jax.random.normal(ks[8], (DEPTH, n_xbc), f32)
    dt0 = jnp.exp(jax.random.uniform(ks[9], (DEPTH, SSM_HEADS), f32, math.log(1e-3), math.log(1e-1)))
    dt_bias = dt0 + jnp.log(-jnp.expm1(-dt0))
    a_log = jnp.log(jax.random.uniform(ks[10], (DEPTH, SSM_HEADS), f32, 1.0, 16.0))
    d_skip = 1.0 + 0.1 * jax.random.normal(ks[11], (DEPTH, SSM_HEADS), f32)
    ssm_norm_g = 1.0 + 0.05 * jax.random.normal(ks[12], (DEPTH, D_B), f32)
    conf_conv_w = jax.random.normal(ks[13], (DEPTH, CONF_K, D_C), f32) * CONF_K ** -0.5
    conf_conv_b = 0.02 * jax.random.normal(ks[14], (DEPTH, D_C), f32)
    conf_ln_g = 1.0 + 0.05 * jax.random.normal(ks[15], (DEPTH, D_C), f32)
    conf_ln_b = 0.02 * jax.random.normal(ks[16], (DEPTH, D_C), f32)
    return {"x": x, "meta": meta, "pre_g": pre_g, "post_g": post_g, "w_in": w_in, "w_out": w_out,
            "conv_a_w": conv_a_w, "ssm_conv_w": ssm_conv_w, "ssm_conv_b": ssm_conv_b,
            "dt_bias": dt_bias, "a_log": a_log, "d_skip": d_skip, "ssm_norm_g": ssm_norm_g,
            "conf_conv_w": conf_conv_w, "conf_conv_b": conf_conv_b,
            "conf_ln_g": conf_ln_g, "conf_ln_b": conf_ln_b}


def reference(x, meta, pre_g, post_g, w_in, w_out, conv_a_w, ssm_conv_w, ssm_conv_b,
              dt_bias, a_log, d_skip, ssm_norm_g, conf_conv_w, conf_conv_b, conf_ln_g, conf_ln_b):
    bsz = x.shape[0]
    meta_b = jnp.broadcast_to(meta.astype(x.dtype)[None], (bsz, N_META, D_MODEL))
    h = jnp.concatenate([meta_b, x], axis=1)
    for i in range(DEPTH):
        m = hybrid_mixer(rmsnorm(h, pre_g[i]), w_in[i], w_out[i], conv_a_w[i], ssm_conv_w[i],
                         ssm_conv_b[i], dt_bias[i], a_log[i], d_skip[i], ssm_norm_g[i],
                         conf_conv_w[i], conf_conv_b[i], conf_ln_g[i], conf_ln_b[i])
        h = h + rmsnorm(m, post_g[i])
    return h[:, N_META:]
```

```python
import functools

import jax
import jax.numpy as jnp
from jax import lax
from jax.experimental import pallas as pl
from jax.experimental.pallas import tpu as pltpu

F32 = jnp.float32
BF16 = jnp.bfloat16

CHUNK = 64
N_META = 16
SSD_PAD = (-N_META) % CHUNK
CONV_A_K = 3
SSM_CONV_K = 4
CONF_K = 31
SSM_HEAD_DIM = 64
SSM_GROUPS = 2
SSM_STATE = 128
NORM_EPS = 1e-6
LN_EPS = 1e-5

LANES = 128
SUBLANES = 8
HALO_SMALL = SUBLANES
HALO_CONF = 4 * SUBLANES
SPLIT_LANES = 16
VMEM_LIMIT_BYTES = 56 * 1024 * 1024
BLOCK_TOKENS = 256


def _dims(d_model):
    d_mix = 2 * d_model
    d_a, d_b, d_c = d_mix // 4, d_mix // 2, d_mix // 4
    heads = d_b // SSM_HEAD_DIM
    gn = SSM_GROUPS * SSM_STATE
    return d_mix, d_a, d_b, d_c, heads, gn


def _split3(q):
    hi = q.astype(BF16).astype(F32)
    r1 = q - hi
    mid = r1.astype(BF16).astype(F32)
    lo = r1 - mid
    return hi, mid, lo


def _pack3(q):
    hi, mid, lo = _split3(q)
    packed = hi + pltpu.roll(mid, SPLIT_LANES, axis=1) + pltpu.roll(lo, 2 * SPLIT_LANES, axis=1)
    return packed.astype(BF16)


def _softplus(x):
    return jnp.maximum(x, 0.0) + jnp.log1p(jnp.exp(-jnp.abs(x)))


def _layer_kernel(n_pad, emit_state, T, d_model,
                  h_ref, tA_ref, tB_ref, tC_ref, S0_ref, pre_g_ref, post_g_ref, w_in_ref, w_out_ref,
                  caw_ref, scw_ref, scb_ref, dtb_ref, alog_ref, dsk_ref, sng_ref,
                  ccw_ref, ccb_ref, lng_ref, lnb_ref, e3_ref, tril3_ref, itile_ref, cmask_ref,
                  *rest):
    if emit_state:
        out_ref, oA_ref, oB_ref, oC_ref, oS_ref = rest[:5]
        scratch = rest[5:]
    else:
        out_ref = rest[0]
        scratch = rest[1:]
    bufA, bufB, bufC, S_scr, ybuf = scratch

    d_mix, d_a, d_b, d_c, heads, gn = _dims(d_model)
    o_az = 3 * d_a
    o_bz = 4 * d_a
    o_bx = o_bz + d_b
    o_dt = o_bx + d_b + 2 * gn
    o_c = o_dt + LANES
    n_xbc = d_b + 2 * gn

    @pl.when(pl.program_id(1) == 0)
    def _():
        bufA[0:HALO_SMALL, :] = tA_ref[...]
        bufB[0:HALO_SMALL, :] = tB_ref[...]
        bufC[0:HALO_CONF, :] = tC_ref[...]
        S_scr[...] = S0_ref[...]

    if n_pad:
        row_ok = lax.broadcasted_iota(jnp.int32, (T, 1), 0) >= n_pad

        def keep_rows(v):
            return jnp.where(row_ok, v, 0.0)
    else:
        def keep_rows(v):
            return v

    h = h_ref[...]
    hn = h * lax.rsqrt(jnp.mean(h * h, axis=-1, keepdims=True) + NORM_EPS) * pre_g_ref[...]
    hn = hn.astype(BF16)

    def proj(lo, hi):
        return jnp.dot(hn, w_in_ref[:, lo:hi], preferred_element_type=F32)

    pA = proj(0, o_bz)
    a_b, a_c, a_x, a_z = (pA[:, k * d_a:(k + 1) * d_a] for k in range(4))
    bufA[HALO_SMALL:HALO_SMALL + T, :] = keep_rows(a_c * a_x)
    conv = None
    for k in range(CONV_A_K):
        off = HALO_SMALL - (CONV_A_K - 1) + k
        term = caw_ref[k:k + 1, :] * bufA[off:off + T, :]
        conv = term if conv is None else conv + term
    ybuf[:, 0:d_a] = (a_b * conv * jax.nn.silu(a_z)).astype(BF16)
    tailA = bufA[T:T + HALO_SMALL, :]
    bufA[0:HALO_SMALL, :] = tailA

    pC = proj(o_c, o_c + 3 * d_c)
    c_a, c_g, c_z = (pC[:, k * d_c:(k + 1) * d_c] for k in range(3))
    bufC[HALO_CONF:HALO_CONF + T, :] = keep_rows(c_a * jax.nn.sigmoid(c_g))
    u = None
    for k in range(CONF_K):
        off = HALO_CONF - (CONF_K - 1) + k
        term = ccw_ref[k:k + 1, :] * bufC[off:off + T, :]
        u = term if u is None else u + term
    u = u + ccb_ref[...]
    mu = jnp.mean(u, axis=-1, keepdims=True)
    uc = u - mu
    var = jnp.mean(uc * uc, axis=-1, keepdims=True)
    u = uc * lax.rsqrt(var + LN_EPS) * lng_ref[...] + lnb_ref[...]
    ybuf[:, d_a + d_b:d_mix] = (jax.nn.silu(u) * jax.nn.silu(c_z)).astype(BF16)
    tailC = bufC[T:T + HALO_CONF, :]
    bufC[0:HALO_CONF, :] = tailC

    b_z = proj(o_bz, o_bx)
    bufB[HALO_SMALL:HALO_SMALL + T, :] = keep_rows(proj(o_bx, o_dt))
    xbc = None
    for k in range(SSM_CONV_K):
        off = HALO_SMALL - (SSM_CONV_K - 1) + k
        term = scw_ref[k:k + 1, :] * bufB[off:off + T, :]
        xbc = term if xbc is None else xbc + term
    xbc = jax.nn.silu(xbc + scb_ref[...])
    tailB = bufB[T:T + HALO_SMALL, :]
    bufB[0:HALO_SMALL, :] = tailB
    xs = xbc[:, 0:d_b]
    bs = xbc[:, d_b:d_b + gn].astype(BF16)
    cs = xbc[:, d_b + gn:n_xbc].astype(BF16)

    head_lane = lax.broadcasted_iota(jnp.int32, (1, LANES), 1) < heads
    dt = _softplus(proj(o_dt, o_c) + dtb_ref[...])
    dt = keep_rows(jnp.where(head_lane, dt, 0.0))
    da = dt * (-jnp.exp(alog_ref[...]))

    e3 = e3_ref[...]
    tril3 = tril3_ref[...]
    itile = itile_ref[...]
    cmask = cmask_ref[...]
    dsk = dsk_ref[...]
    sng = sng_ref[...]
    pair_lo = lax.broadcasted_iota(jnp.int32, (1, LANES), 1) < SSM_HEAD_DIM
    gw = d_b // SSM_GROUPS
    for c in range(T // CHUNK):
        r0 = c * CHUNK
        x = xs[r0:r0 + CHUNK, :]
        hi, mid, lo = _split3(da[r0:r0 + CHUNK, :])
        acum = jnp.dot(tril3, jnp.concatenate([hi, mid, lo], axis=0).astype(BF16),
                       preferred_element_type=F32)
        ex_acum = jnp.dot(_pack3(acum), e3, preferred_element_type=F32)
        ex_dt = jnp.dot(_pack3(dt[r0:r0 + CHUNK, :]), e3, preferred_element_type=F32)
        eaw = jnp.exp(ex_acum)
        last = ex_acum[CHUNK - 1:CHUNK, :]
        dsw = jnp.exp(last - ex_acum)
        cdrow = eaw[CHUNK - 1:CHUNK, :]
        xdt = x * ex_dt
        xdt_b = xdt.astype(BF16)
        xds_b = (xdt * dsw).astype(BF16)
        rrow = jnp.sum(ex_acum * itile, axis=0, keepdims=True)
        decay = jnp.exp(ex_acum - rrow + cmask)
        y_parts = []
        for g in range(SSM_GROUPS):
            cm_g = cs[r0:r0 + CHUNK, g * SSM_STATE:(g + 1) * SSM_STATE]
            bm_g = bs[r0:r0 + CHUNK, g * SSM_STATE:(g + 1) * SSM_STATE]
            bm2 = jnp.concatenate([bm_g, bm_g], axis=0)
            cb2 = lax.dot_general(cm_g, bm2, (((1,), (1,)), ((), ())), preferred_element_type=F32)
            st = S_scr[:, g * gw:(g + 1) * gw]
            y_off = jnp.dot(cm_g, st.astype(BF16), preferred_element_type=F32)
            y_off = y_off * eaw[:, g * gw:(g + 1) * gw]
            for j in range(gw // LANES):
                lo_l = g * gw + j * LANES
                m_pair = (cb2 * decay[:, lo_l:lo_l + LANES]).astype(BF16)
                xp = xdt_b[:, lo_l:lo_l + LANES]
                zero = jnp.zeros_like(xp)
                rhs = jnp.concatenate([jnp.where(pair_lo, xp, zero), jnp.where(pair_lo, zero, xp)], axis=0)
                y_d = jnp.dot(m_pair, rhs, preferred_element_type=F32)
                y_parts.append(y_d + y_off[:, j * LANES:(j + 1) * LANES])
            upd = lax.dot_general(bm_g, xds_b[:, g * gw:(g + 1) * gw], (((0,), (0,)), ((), ())),
                                  preferred_element_type=F32)
            S_scr[:, g * gw:(g + 1) * gw] = st * cdrow[:, g * gw:(g + 1) * gw] + upd
        y = jnp.concatenate(y_parts, axis=1) + x * dsk
        y = y * jax.nn.silu(b_z[r0:r0 + CHUNK, :])
        y = y * lax.rsqrt(jnp.mean(y * y, axis=-1, keepdims=True) + NORM_EPS) * sng
        ybuf[r0:r0 + CHUNK, d_a:d_a + d_b] = y.astype(BF16)

    m = jnp.dot(ybuf[...], w_out_ref[...], preferred_element_type=F32)
    m = m * lax.rsqrt(jnp.mean(m * m, axis=-1, keepdims=True) + NORM_EPS) * post_g_ref[...]
    out_ref[...] = h + m

    if emit_state:
        oA_ref[...] = tailA
        oB_ref[...] = tailB
        oC_ref[...] = tailC
        oS_ref[...] = S_scr[...]


def _const_spec(shape):
    return pl.BlockSpec(shape, lambda b, i: (0,) * len(shape))


def _layer_call(h, state, wts, consts, *, T, n_pad, emit_state):
    B, L, d_model = h.shape
    d_mix, d_a, d_b, d_c, heads, gn = _dims(d_model)
    n_xbc = d_b + 2 * gn
    assert L % T == 0 and T % CHUNK == 0
    grid = (B, L // T)
    h_spec = pl.BlockSpec((None, T, d_model), lambda b, i: (b, i, 0))
    ins = [h, *state, *wts, *consts]
    in_specs = [h_spec] + [_const_spec(a.shape) for a in ins[1:]]
    out_shape = [jax.ShapeDtypeStruct(h.shape, F32)]
    out_specs = [h_spec]
    if emit_state:
        for a in state:
            out_shape.append(jax.ShapeDtypeStruct(a.shape, F32))
            out_specs.append(_const_spec(a.shape))
    scratch = [
        pltpu.VMEM((HALO_SMALL + T, d_a), F32),
        pltpu.VMEM((HALO_SMALL + T, n_xbc), F32),
        pltpu.VMEM((HALO_CONF + T, d_c), F32),
        pltpu.VMEM((SSM_STATE, d_b), F32),
        pltpu.VMEM((T, d_mix), BF16),
    ]
    res = pl.pallas_call(
        functools.partial(_layer_kernel, n_pad, emit_state, T, d_model),
        grid=grid,
        in_specs=in_specs,
        out_specs=out_specs,
        out_shape=out_shape,
        scratch_shapes=scratch,
        compiler_params=pltpu.CompilerParams(
            dimension_semantics=("arbitrary", "arbitrary"),
            vmem_limit_bytes=VMEM_LIMIT_BYTES),
        name="meta_layer" if emit_state else "trunk_layer",
    )(*ins)
    return res


def _make_consts(d_model):
    d_mix, d_a, d_b, d_c, heads, gn = _dims(d_model)
    lane = jnp.arange(d_b)
    row = jnp.arange(LANES)
    e3 = ((row[:, None] % SPLIT_LANES == lane[None, :] // SSM_HEAD_DIM)
          & (row[:, None] < 3 * SPLIT_LANES)).astype(BF16)
    t = jnp.arange(CHUNK)
    tril = (t[None, :] <= t[:, None]).astype(BF16)
    tril3 = jnp.concatenate([tril, tril, tril], axis=1)
    s_of_lane = lane % SSM_HEAD_DIM
    itile = (s_of_lane[None, :] == t[:, None]).astype(F32)
    cmask = jnp.where(s_of_lane[None, :] <= t[:, None], 0.0, -jnp.inf).astype(F32)
    return [e3, tril3, itile, cmask]


def _layer_weights(i, d_model, pre_g, post_g, w_in, w_out, conv_a_w, ssm_conv_w, ssm_conv_b, dt_bias,
                   a_log, d_skip, ssm_norm_g, conf_conv_w, conf_conv_b, conf_ln_g, conf_ln_b):
    d_mix, d_a, d_b, d_c, heads, gn = _dims(d_model)
    o_dt = 4 * d_a + 2 * d_b + 2 * gn
    w = w_in[i]
    w_p = jnp.concatenate(
        [w[:, :o_dt], jnp.pad(w[:, o_dt:o_dt + heads], ((0, 0), (0, LANES - heads))), w[:, o_dt + heads:]],
        axis=1).astype(BF16)
    pad_h = (0, LANES - heads)
    return [
        pre_g[i][None], post_g[i][None], w_p, w_out[i].astype(BF16),
        conv_a_w[i], ssm_conv_w[i], ssm_conv_b[i][None],
        jnp.pad(dt_bias[i], pad_h)[None], jnp.pad(a_log[i], pad_h)[None],
        jnp.repeat(d_skip[i], SSM_HEAD_DIM)[None], ssm_norm_g[i][None],
        conf_conv_w[i], conf_conv_b[i][None], conf_ln_g[i][None], conf_ln_b[i][None],
    ]


def kernel(x, meta, pre_g, post_g, w_in, w_out, conv_a_w, ssm_conv_w, ssm_conv_b, dt_bias, a_log, d_skip,
           ssm_norm_g, conf_conv_w, conf_conv_b, conf_ln_g, conf_ln_b):
    B, L, d_model = x.shape
    d_mix, d_a, d_b, d_c, heads, gn = _dims(d_model)
    depth = w_in.shape[0]
    assert heads <= SPLIT_LANES and meta.shape[0] == N_META
    consts = _make_consts(d_model)
    zero_state = [
        jnp.zeros((HALO_SMALL, d_a), F32),
        jnp.zeros((HALO_SMALL, d_b + 2 * gn), F32),
        jnp.zeros((HALO_CONF, d_c), F32),
        jnp.zeros((SSM_STATE, d_b), F32),
    ]
    T = min(BLOCK_TOKENS, L)
    h = x
    hm = jnp.pad(meta.astype(x.dtype), ((SSD_PAD, 0), (0, 0)))[None]
    for i in range(depth):
        wts = _layer_weights(i, d_model, pre_g, post_g, w_in, w_out, conv_a_w, ssm_conv_w, ssm_conv_b,
                             dt_bias, a_log, d_skip, ssm_norm_g, conf_conv_w, conf_conv_b, conf_ln_g,
                             conf_ln_b)
        hm, *state = _layer_call(hm, zero_state, wts, consts, T=CHUNK, n_pad=SSD_PAD, emit_state=True)
        h = _layer_call(h, state, wts, consts, T=T, n_pad=0, emit_state=False)[0]
    return h
```

```python
import functools

import jax
import jax.numpy as jnp
from jax import lax
from jax.experimental import pallas as pl
from jax.experimental.pallas import tpu as pltpu

F32 = jnp.float32
BF16 = jnp.bfloat16

CHUNK = 64
N_META = 16
SSD_PAD = (-N_META) % CHUNK
CONV_A_K = 3
SSM_CONV_K = 4
CONF_K = 31
SSM_HEAD_DIM = 64
SSM_GROUPS = 2
SSM_STATE = 128
NORM_EPS = 1e-6
LN_EPS = 1e-5

LANES = 128
SUBLANES = 8
HALO_SMALL = SUBLANES
HALO_CONF = 4 * SUBLANES
SPLIT_LANES = 16
VMEM_LIMIT_BYTES = 56 * 1024 * 1024
BLOCK_TOKENS = 256


def _dims(d_model):
    d_mix = 2 * d_model
    d_a, d_b, d_c = d_mix // 4, d_mix // 2, d_mix // 4
    heads = d_b // SSM_HEAD_DIM
    gn = SSM_GROUPS * SSM_STATE
    return d_mix, d_a, d_b, d_c, heads, gn


def _split3(q):
    hi = q.astype(BF16).astype(F32)
    r1 = q - hi
    mid = r1.astype(BF16).astype(F32)
    lo = r1 - mid
    return hi, mid, lo


def _pack3(q):
    hi, mid, lo = _split3(q)
    packed = hi + pltpu.roll(mid, SPLIT_LANES, axis=1) + pltpu.roll(lo, 2 * SPLIT_LANES, axis=1)
    return packed.astype(BF16)


def _softplus(x):
    return jnp.maximum(x, 0.0) + jnp.log1p(jnp.exp(-jnp.abs(x)))


def _to_slabs(slab_ref, halo, v):
    T = v.shape[0]
    for j in range(slab_ref.shape[0]):
        slab_ref[j, halo:halo + T, :] = v[:, j * LANES:(j + 1) * LANES]


def _from_slabs(slab_ref, lo=0, hi=None):
    hi = slab_ref.shape[0] if hi is None else hi
    return jnp.concatenate([slab_ref[j] for j in range(lo, hi)], axis=1)


def _slab_conv(in_ref, out_ref, w_ref, n_taps, halo, T, bias_ref=None, act=None):
    group = 2 * SUBLANES
    for j in range(in_ref.shape[0]):
        lanes = slice(j * LANES, (j + 1) * LANES)
        for g in range(T // group):
            for ph in range(2):
                acc = None
                for k in range(n_taps):
                    start = g * group + ph + halo - (n_taps - 1) + k
                    term = w_ref[k:k + 1, lanes] * in_ref[j, pl.ds(start, SUBLANES, stride=2), :]
                    acc = term if acc is None else acc + term
                if bias_ref is not None:
                    acc = acc + bias_ref[:, lanes]
                if act is not None:
                    acc = act(acc)
                out_ref[j, pl.ds(g * group + ph, SUBLANES, stride=2), :] = acc


def _layer_kernel(n_pad, emit_state, T, d_model,
                  h_ref, tA_ref, tB_ref, tC_ref, S0_ref, pre_g_ref, post_g_ref, w_in_ref, w_out_ref,
                  caw_ref, scw_ref, scb_ref, dtb_ref, alog_ref, dsk_ref, sng_ref,
                  ccw_ref, ccb_ref, lng_ref, lnb_ref, e3_ref, tril3_ref, itile_ref, cmask_ref,
                  *rest):
    if emit_state:
        out_ref, oA_ref, oB_ref, oC_ref, oS_ref = rest[:5]
        scratch = rest[5:]
    else:
        out_ref = rest[0]
        scratch = rest[1:]
    bufA, bufB, bufC, outA, outB, outC, S_scr, ybuf = scratch

    d_mix, d_a, d_b, d_c, heads, gn = _dims(d_model)
    o_az = 3 * d_a
    o_bz = 4 * d_a
    o_bx = o_bz + d_b
    o_dt = o_bx + d_b + 2 * gn
    o_c = o_dt + LANES
    n_xbc = d_b + 2 * gn

    @pl.when(pl.program_id(1) == 0)
    def _():
        bufA[:, 0:HALO_SMALL, :] = tA_ref[...]
        bufB[:, 0:HALO_SMALL, :] = tB_ref[...]
        bufC[:, 0:HALO_CONF, :] = tC_ref[...]
        S_scr[...] = S0_ref[...]

    if n_pad:
        row_ok = lax.broadcasted_iota(jnp.int32, (T, 1), 0) >= n_pad

        def keep_rows(v):
            return jnp.where(row_ok, v, 0.0)
    else:
        def keep_rows(v):
            return v

    h = h_ref[...]
    hn = h * lax.rsqrt(jnp.mean(h * h, axis=-1, keepdims=True) + NORM_EPS) * pre_g_ref[...]
    hn = hn.astype(BF16)

    def proj(lo, hi):
        return jnp.dot(hn, w_in_ref[:, lo:hi], preferred_element_type=F32)

    pA = proj(0, o_bz)
    a_b, a_c, a_x, a_z = (pA[:, k * d_a:(k + 1) * d_a] for k in range(4))
    _to_slabs(bufA, HALO_SMALL, keep_rows(a_c * a_x))
    _slab_conv(bufA, outA, caw_ref, CONV_A_K, HALO_SMALL, T)
    conv = _from_slabs(outA)
    ybuf[:, 0:d_a] = (a_b * conv * jax.nn.silu(a_z)).astype(BF16)
    tailA = bufA[:, T:T + HALO_SMALL, :]
    bufA[:, 0:HALO_SMALL, :] = tailA

    pC = proj(o_c, o_c + 3 * d_c)
    c_a, c_g, c_z = (pC[:, k * d_c:(k + 1) * d_c] for k in range(3))
    _to_slabs(bufC, HALO_CONF, keep_rows(c_a * jax.nn.sigmoid(c_g)))
    _slab_conv(bufC, outC, ccw_ref, CONF_K, HALO_CONF, T, bias_ref=ccb_ref)
    u = _from_slabs(outC)
    mu = jnp.mean(u, axis=-1, keepdims=True)
    uc = u - mu
    var = jnp.mean(uc * uc, axis=-1, keepdims=True)
    u = uc * lax.rsqrt(var + LN_EPS) * lng_ref[...] + lnb_ref[...]
    ybuf[:, d_a + d_b:d_mix] = (jax.nn.silu(u) * jax.nn.silu(c_z)).astype(BF16)
    tailC = bufC[:, T:T + HALO_CONF, :]
    bufC[:, 0:HALO_CONF, :] = tailC

    b_z = proj(o_bz, o_bx)
    _to_slabs(bufB, HALO_SMALL, keep_rows(proj(o_bx, o_dt)))
    _slab_conv(bufB, outB, scw_ref, SSM_CONV_K, HALO_SMALL, T, bias_ref=scb_ref, act=jax.nn.silu)
    tailB = bufB[:, T:T + HALO_SMALL, :]
    bufB[:, 0:HALO_SMALL, :] = tailB
    xs = _from_slabs(outB, 0, d_b // LANES)
    bs = _from_slabs(outB, d_b // LANES, (d_b + gn) // LANES).astype(BF16)
    cs = _from_slabs(outB, (d_b + gn) // LANES, n_xbc // LANES).astype(BF16)

    head_lane = lax.broadcasted_iota(jnp.int32, (1, LANES), 1) < heads
    dt = _softplus(proj(o_dt, o_c) + dtb_ref[...])
    dt = keep_rows(jnp.where(head_lane, dt, 0.0))
    da = dt * (-jnp.exp(alog_ref[...]))

    e3 = e3_ref[...]
    tril3 = tril3_ref[...]
    itile = itile_ref[...]
    cmask = cmask_ref[...]
    dsk = dsk_ref[...]
    sng = sng_ref[...]
    pair_lo = lax.broadcasted_iota(jnp.int32, (1, LANES), 1) < SSM_HEAD_DIM
    gw = d_b // SSM_GROUPS
    for c in range(T // CHUNK):
        r0 = c * CHUNK
        x = xs[r0:r0 + CHUNK, :]
        hi, mid, lo = _split3(da[r0:r0 + CHUNK, :])
        acum = jnp.dot(tril3, jnp.concatenate([hi, mid, lo], axis=0).astype(BF16),
                       preferred_element_type=F32)
        ex_acum = jnp.dot(_pack3(acum), e3, preferred_element_type=F32)
        ex_dt = jnp.dot(_pack3(dt[r0:r0 + CHUNK, :]), e3, preferred_element_type=F32)
        eaw = jnp.exp(ex_acum)
        last = ex_acum[CHUNK - 1:CHUNK, :]
        dsw = jnp.exp(last - ex_acum)
        cdrow = eaw[CHUNK - 1:CHUNK, :]
        xdt = x * ex_dt
        xdt_b = xdt.astype(BF16)
        xds_b = (xdt * dsw).astype(BF16)
        rrow = jnp.sum(ex_acum * itile, axis=0, keepdims=True)
        decay = jnp.exp(ex_acum - rrow + cmask)
        y_parts = []
        for g in range(SSM_GROUPS):
            cm_g = cs[r0:r0 + CHUNK, g * SSM_STATE:(g + 1) * SSM_STATE]
            bm_g = bs[r0:r0 + CHUNK, g * SSM_STATE:(g + 1) * SSM_STATE]
            bm2 = jnp.concatenate([bm_g, bm_g], axis=0)
            cb2 = lax.dot_general(cm_g, bm2, (((1,), (1,)), ((), ())), preferred_element_type=F32)
            st = S_scr[:, g * gw:(g + 1) * gw]
            y_off = jnp.dot(cm_g, st.astype(BF16), preferred_element_type=F32)
            y_off = y_off * eaw[:, g * gw:(g + 1) * gw]
            for j in range(gw // LANES):
                lo_l = g * gw + j * LANES
                m_pair = (cb2 * decay[:, lo_l:lo_l + LANES]).astype(BF16)
                xp = xdt_b[:, lo_l:lo_l + LANES]
                zero = jnp.zeros_like(xp)
                rhs = jnp.concatenate([jnp.where(pair_lo, xp, zero), jnp.where(pair_lo, zero, xp)], axis=0)
                y_d = jnp.dot(m_pair, rhs, preferred_element_type=F32)
                y_parts.append(y_d + y_off[:, j * LANES:(j + 1) * LANES])
            upd = lax.dot_general(bm_g, xds_b[:, g * gw:(g + 1) * gw], (((0,), (0,)), ((), ())),
                                  preferred_element_type=F32)
            S_scr[:, g * gw:(g + 1) * gw] = st * cdrow[:, g * gw:(g + 1) * gw] + upd
        y = jnp.concatenate(y_parts, axis=1) + x * dsk
        y = y * jax.nn.silu(b_z[r0:r0 + CHUNK, :])
        y = y * lax.rsqrt(jnp.mean(y * y, axis=-1, keepdims=True) + NORM_EPS) * sng
        ybuf[r0:r0 + CHUNK, d_a:d_a + d_b] = y.astype(BF16)

    m = jnp.dot(ybuf[...], w_out_ref[...], preferred_element_type=F32)
    m = m * lax.rsqrt(jnp.mean(m * m, axis=-1, keepdims=True) + NORM_EPS) * post_g_ref[...]
    out_ref[...] = h + m

    if emit_state:
        oA_ref[...] = tailA
        oB_ref[...] = tailB
        oC_ref[...] = tailC
        oS_ref[...] = S_scr[...]


def _const_spec(shape):
    return pl.BlockSpec(shape, lambda b, i: (0,) * len(shape))


def _layer_call(h, state, wts, consts, *, T, n_pad, emit_state):
    B, L, d_model = h.shape
    d_mix, d_a, d_b, d_c, heads, gn = _dims(d_model)
    n_xbc = d_b + 2 * gn
    assert L % T == 0 and T % CHUNK == 0
    grid = (B, L // T)
    h_spec = pl.BlockSpec((None, T, d_model), lambda b, i: (b, i, 0))
    ins = [h, *state, *wts, *consts]
    in_specs = [h_spec] + [_const_spec(a.shape) for a in ins[1:]]
    out_shape = [jax.ShapeDtypeStruct(h.shape, F32)]
    out_specs = [h_spec]
    if emit_state:
        for a in state:
            out_shape.append(jax.ShapeDtypeStruct(a.shape, F32))
            out_specs.append(_const_spec(a.shape))
    scratch = [
        pltpu.VMEM((d_a // LANES, HALO_SMALL + T, LANES), F32),
        pltpu.VMEM((n_xbc // LANES, HALO_SMALL + T, LANES), F32),
        pltpu.VMEM((d_c // LANES, HALO_CONF + T, LANES), F32),
        pltpu.VMEM((d_a // LANES, T, LANES), F32),
        pltpu.VMEM((n_xbc // LANES, T, LANES), F32),
        pltpu.VMEM((d_c // LANES, T, LANES), F32),
        pltpu.VMEM((SSM_STATE, d_b), F32),
        pltpu.VMEM((T, d_mix), BF16),
    ]
    res = pl.pallas_call(
        functools.partial(_layer_kernel, n_pad, emit_state, T, d_model),
        grid=grid,
        in_specs=in_specs,
        out_specs=out_specs,
        out_shape=out_shape,
        scratch_shapes=scratch,
        compiler_params=pltpu.CompilerParams(
            dimension_semantics=("arbitrary", "arbitrary"),
            vmem_limit_bytes=VMEM_LIMIT_BYTES),
        name="meta_layer" if emit_state else "trunk_layer",
    )(*ins)
    return res


def _make_consts(d_model):
    d_mix, d_a, d_b, d_c, heads, gn = _dims(d_model)
    lane = jnp.arange(d_b)
    row = jnp.arange(LANES)
    e3 = ((row[:, None] % SPLIT_LANES == lane[None, :] // SSM_HEAD_DIM)
          & (row[:, None] < 3 * SPLIT_LANES)).astype(BF16)
    t = jnp.arange(CHUNK)
    tril = (t[None, :] <= t[:, None]).astype(BF16)
    tril3 = jnp.concatenate([tril, tril, tril], axis=1)
    s_of_lane = lane % SSM_HEAD_DIM
    itile = (s_of_lane[None, :] == t[:, None]).astype(F32)
    cmask = jnp.where(s_of_lane[None, :] <= t[:, None], 0.0, -jnp.inf).astype(F32)
    return [e3, tril3, itile, cmask]


def _layer_weights(i, d_model, pre_g, post_g, w_in, w_out, conv_a_w, ssm_conv_w, ssm_conv_b, dt_bias,
                   a_log, d_skip, ssm_norm_g, conf_conv_w, conf_conv_b, conf_ln_g, conf_ln_b):
    d_mix, d_a, d_b, d_c, heads, gn = _dims(d_model)
    o_dt = 4 * d_a + 2 * d_b + 2 * gn
    w = w_in[i]
    w_p = jnp.concatenate(
        [w[:, :o_dt], jnp.pad(w[:, o_dt:o_dt + heads], ((0, 0), (0, LANES - heads))), w[:, o_dt + heads:]],
        axis=1).astype(BF16)
    pad_h = (0, LANES - heads)
    return [
        pre_g[i][None], post_g[i][None], w_p, w_out[i].astype(BF16),
        conv_a_w[i], ssm_conv_w[i], ssm_conv_b[i][None],
        jnp.pad(dt_bias[i], pad_h)[None], jnp.pad(a_log[i], pad_h)[None],
        jnp.repeat(d_skip[i], SSM_HEAD_DIM)[None], ssm_norm_g[i][None],
        conf_conv_w[i], conf_conv_b[i][None], conf_ln_g[i][None], conf_ln_b[i][None],
    ]


def kernel(x, meta, pre_g, post_g, w_in, w_out, conv_a_w, ssm_conv_w, ssm_conv_b, dt_bias, a_log, d_skip,
           ssm_norm_g, conf_conv_w, conf_conv_b, conf_ln_g, conf_ln_b):
    B, L, d_model = x.shape
    d_mix, d_a, d_b, d_c, heads, gn = _dims(d_model)
    depth = w_in.shape[0]
    assert heads <= SPLIT_LANES and meta.shape[0] == N_META
    consts = _make_consts(d_model)
    zero_state = [
        jnp.zeros((d_a // LANES, HALO_SMALL, LANES), F32),
        jnp.zeros(((d_b + 2 * gn) // LANES, HALO_SMALL, LANES), F32),
        jnp.zeros((d_c // LANES, HALO_CONF, LANES), F32),
        jnp.zeros((SSM_STATE, d_b), F32),
    ]
    T = min(BLOCK_TOKENS, L)
    h = x
    hm = jnp.pad(meta.astype(x.dtype), ((SSD_PAD, 0), (0, 0)))[None]
    for i in range(depth):
        wts = _layer_weights(i, d_model, pre_g, post_g, w_in, w_out, conv_a_w, ssm_conv_w, ssm_conv_b,
                             dt_bias, a_log, d_skip, ssm_norm_g, conf_conv_w, conf_conv_b, conf_ln_g,
                             conf_ln_b)
        hm, *state = _layer_call(hm, zero_state, wts, consts, T=CHUNK, n_pad=SSD_PAD, emit_state=True)
        h = _layer_call(h, state, wts, consts, T=T, n_pad=0, emit_state=False)[0]
    return h
```
